```python
import jax, jax.numpy as jnp
from jax import lax
import numpy as np

D_MODEL = 2048
BATCH = 1
SEQ = 8192
DEPTH = 1

PLE_DIM = 256
EPS = 1e-6

SSM_HEAD_DIM = 64
SSM_INNER = D_MODEL
SSM_HEADS = SSM_INNER // SSM_HEAD_DIM
SSM_GROUPS = 4
SSM_STATE = 128
SSM_CONV = 4
SSM_CHUNK = 128
SSM_HPG = SSM_HEADS // SSM_GROUPS
SSM_CONV_CH = SSM_INNER + 2 * SSM_GROUPS * SSM_STATE

ATT_HEAD_DIM = 64
ATT_HEADS = D_MODEL // ATT_HEAD_DIM
ATT_KV_HEADS = ATT_HEADS // 4
ATT_REP = ATT_HEADS // ATT_KV_HEADS
ATT_INNER = ATT_HEADS * ATT_HEAD_DIM
ATT_KV_DIM = ATT_KV_HEADS * ATT_HEAD_DIM
WINDOW = 128
ROPE_THETA = 500000.0
ROPE_DIM = ATT_HEAD_DIM // 4

MIX_WIDTH = SSM_INNER + ATT_INNER
IN_WIDTH = SSM_INNER + SSM_CONV_CH + SSM_HEADS + ATT_INNER + 2 * ATT_KV_DIM

D_FF = 256 * ((8 * D_MODEL // 3 + 255) // 256)
FFN_CONV = 3

kernel_name = 'hymba_ssd_swa_convffn_ple'


def _rmsnorm(x, g):
    xf = x.astype(jnp.float32)
    y = xf * lax.rsqrt(jnp.mean(xf * xf, axis=-1, keepdims=True) + EPS)
    return (y * g.astype(jnp.float32)).astype(x.dtype)


def _causal_dwconv(x, w, b):
    k = w.shape[0]
    y = lax.conv_general_dilated(x, w[:, None, :].astype(x.dtype), window_strides=(1,),
                                 padding=[(k - 1, 0)],
                                 dimension_numbers=('NWC', 'WIO', 'NWC'),
                                 feature_group_count=x.shape[-1])
    return y + b.astype(x.dtype)


def _ssd(xs, dt, a, bm, cm):
    b, s = xs.shape[0], xs.shape[1]
    nc, L = s // SSM_CHUNK, SSM_CHUNK
    G, E, P, N = SSM_GROUPS, SSM_HPG, SSM_HEAD_DIM, SSM_STATE
    X = (xs.astype(jnp.float32) * dt[..., None]).reshape(b, nc, L, G, E, P)
    a_dt = (dt * a).reshape(b, nc, L, G, E).transpose(0, 3, 4, 1, 2)
    Bc = bm.astype(jnp.float32).reshape(b, nc, L, G, N)
    Cc = cm.astype(jnp.float32).reshape(b, nc, L, G, N)
    a_cs = jnp.cumsum(a_dt, axis=-1)
    tril = jnp.tril(jnp.ones((L, L), dtype=bool))
    seg = a_cs[..., :, None] - a_cs[..., None, :]
    decay_in = jnp.exp(jnp.where(tril, seg, -jnp.inf))
    cb = jnp.einsum('bclgn,bcsgn->bgcls', Cc, Bc)
    y_diag = jnp.einsum('bgecls,bcsgep->bclgep', cb[:, :, None] * decay_in, X)
    decay_states = jnp.exp(a_cs[..., -1:] - a_cs)
    states = jnp.einsum('bclgn,bgecl,bclgep->bcgepn', Bc, decay_states, X)
    chunk_decay = jnp.exp(a_cs[..., -1])

    def step(h, inp):
        st, d = inp
        return h * d[..., None, None] + st, h

    h0 = jnp.zeros((b, G, E, P, N), jnp.float32)
    _, prev = lax.scan(step, h0, (jnp.moveaxis(states, 1, 0), jnp.moveaxis(chunk_decay, -1, 0)))
    prev = jnp.moveaxis(prev, 0, 1)
    y_off = jnp.einsum('bclgn,bcgepn,bgecl->bclgep', Cc, prev, jnp.exp(a_cs))
    return (y_diag + y_off).reshape(b, s, SSM_HEADS, P)


def _partial_rope(t, cos, sin):
    half = ROPE_DIM // 2
    t1, t2 = t[..., :half], t[..., half:ROPE_DIM]
    return jnp.concatenate([t1 * cos - t2 * sin, t2 * cos + t1 * sin, t[..., ROPE_DIM:]], axis=-1)


def _swa_sinks(q, k, v, sinks):
    b, s = q.shape[0], q.shape[1]
    blk = WINDOW
    nb = s // blk
    scale = ATT_HEAD_DIM ** -0.5
    qb = q.astype(jnp.float32).reshape(b, nb, blk, ATT_KV_HEADS, ATT_REP, ATT_HEAD_DIM)

    def band(t):
        tp = jnp.pad(t.astype(jnp.float32), ((0, 0), (blk, 0), (0, 0), (0, 0)))
        tp = tp.reshape(b, nb + 1, blk, ATT_KV_HEADS, ATT_HEAD_DIM)
        return jnp.concatenate([tp[:, :-1], tp[:, 1:]], axis=2)

    kb, vb = band(k), band(v)
    sc = jnp.einsum('bnqhrd,bnkhd->bnhrqk', qb, kb) * scale
    qi = jnp.arange(blk)[:, None]
    kj = jnp.arange(2 * blk)[None, :]
    diff = qi + blk - kj
    in_band = (diff >= 0) & (diff < WINDOW)
    valid = (jnp.arange(nb)[:, None] * blk + kj - blk) >= 0
    mask = in_band[None] & valid[:, None, :]
    sc = jnp.where(mask[None, :, None, None], sc, -jnp.inf)
    sink = sinks.astype(jnp.float32).reshape(ATT_KV_HEADS, ATT_REP)[None, None, :, :, None, None]
    m = jnp.maximum(jnp.max(sc, axis=-1, keepdims=True), sink)
    e = jnp.exp(sc - m)
    pr = e / (jnp.sum(e, axis=-1, keepdims=True) + jnp.exp(sink - m))
    o = jnp.einsum('bnhrqk,bnkhd->bnqhrd', pr, vb)
    return o.reshape(b, s, ATT_INNER).astype(q.dtype)


def setup_inputs(seed: int = 0) -> dict:
    key = jax.random.key(seed)
    ks = jax.random.split(key, 24)
    f32 = jnp.float32
    nrm = lambda k, shape, sc: jax.random.normal(k, shape, f32) * sc
    gain = lambda k, shape: 1.0 + 0.02 * jax.random.normal(k, shape, f32)
    dt0 = jnp.exp(jax.random.uniform(ks[5], (DEPTH, SSM_HEADS), f32) * (np.log(0.1) - np.log(0.001)) + np.log(0.001))
    return {
        'x': nrm(ks[0], (BATCH, SEQ, D_MODEL), 1.0),
        'p': nrm(ks[1], (DEPTH, BATCH, SEQ, PLE_DIM), 1.0),
        'positions': jnp.broadcast_to(jnp.arange(SEQ, dtype=jnp.int32)[None, :], (BATCH, SEQ)),
        'attn_norm_g': gain(ks[2], (DEPTH, D_MODEL)),
        'w_in': nrm(ks[3], (DEPTH, D_MODEL, IN_WIDTH), D_MODEL ** -0.5),
        'conv_w': nrm(ks[4], (DEPTH, SSM_CONV, SSM_CONV_CH), SSM_CONV ** -0.5),
        'conv_b': nrm(ks[6], (DEPTH, SSM_CONV_CH), 0.02),
        'dt_bias': dt0 + jnp.log(-jnp.expm1(-dt0)),
        'a_log': jnp.log(jax.random.uniform(ks[7], (DEPTH, SSM_HEADS), f32, 1.0, 16.0)),
        'd_skip': gain(ks[8], (DEPTH, SSM_HEADS)),
        'ssm_norm_g': gain(ks[9], (DEPTH, SSM_INNER)),
        'q_norm_g': gain(ks[10], (DEPTH, ATT_HEAD_DIM)),
        'k_norm_g': gain(ks[11], (DEPTH, ATT_HEAD_DIM)),
        'sinks': nrm(ks[12], (DEPTH, ATT_HEADS), 0.5),
        'w_out': nrm(ks[13], (DEPTH, MIX_WIDTH, D_MODEL), MIX_WIDTH ** -0.5),
        'ffn_norm_g': gain(ks[14], (DEPTH, D_MODEL)),
        'w_up': nrm(ks[15], (DEPTH, D_MODEL, 2 * D_FF), D_MODEL ** -0.5),
        'ffn_conv_w': nrm(ks[16], (DEPTH, FFN_CONV, 2 * D_FF), FFN_CONV ** -0.5),
        'ffn_conv_b': nrm(ks[17], (DEPTH, 2 * D_FF), 0.02),
        'w_down': nrm(ks[18], (DEPTH, D_FF, D_MODEL), D_FF ** -0.5),
        'ple_norm_g': gain(ks[19], (DEPTH, D_MODEL)),
        'w_ple_gate': nrm(ks[20], (DEPTH, D_MODEL, D_MODEL), D_MODEL ** -0.5),
        'w_ple_proj': nrm(ks[21], (DEPTH, PLE_DIM, D_MODEL), PLE_DIM ** -0.5),
        'ple_post_g': gain(ks[22], (DEPTH, D_MODEL)),
    }


def reference(x, p, positions, attn_norm_g, w_in, conv_w, conv_b, dt_bias, a_log, d_skip,
              ssm_norm_g, q_norm_g, k_norm_g, sinks, w_out, ffn_norm_g, w_up, ffn_conv_w,
              ffn_conv_b, w_down, ple_norm_g, w_ple_gate, w_ple_proj, ple_post_g):
    b, s = x.shape[0], x.shape[1]
    inv_freq = ROPE_THETA ** (-jnp.arange(0, ROPE_DIM, 2, dtype=jnp.float32) / ROPE_DIM)
    ang = positions.astype(jnp.float32)[..., None] * inv_freq
    cos = jnp.cos(ang)[:, :, None, :].astype(x.dtype)
    sin = jnp.sin(ang)[:, :, None, :].astype(x.dtype)
    o1 = SSM_INNER
    o2 = o1 + SSM_CONV_CH
    o3 = o2 + SSM_HEADS
    o4 = o3 + ATT_INNER
    o5 = o4 + ATT_KV_DIM
    h = x
    for i in range(DEPTH):
        a_in = _rmsnorm(h, attn_norm_g[i])
        proj = a_in @ w_in[i]
        z, xbc, dt_raw, q, k, v = jnp.split(proj, [o1, o2, o3, o4, o5], axis=-1)
        xbc = jax.nn.silu(_causal_dwconv(xbc, conv_w[i], conv_b[i]))
        xs, bm, cm = jnp.split(xbc, [SSM_INNER, SSM_INNER + SSM_GROUPS * SSM_STATE], axis=-1)
        dt = jax.nn.softplus(dt_raw.astype(jnp.float32) + dt_bias[i].astype(jnp.float32))
        a_neg = -jnp.exp(a_log[i].astype(jnp.float32))
        xs_h = xs.reshape(b, s, SSM_HEADS, SSM_HEAD_DIM)
        y = _ssd(xs_h, dt, a_neg, bm.reshape(b, s, SSM_GROUPS, SSM_STATE),
                 cm.reshape(b, s, SSM_GROUPS, SSM_STATE))
        y = y + d_skip[i].astype(jnp.float32)[:, None] * xs_h.astype(jnp.float32)
        y = y.reshape(b, s, SSM_INNER) * jax.nn.silu(z.astype(jnp.float32))
        y_ssm = _rmsnorm(y.reshape(b, s, SSM_GROUPS, SSM_INNER // SSM_GROUPS),
                         ssm_norm_g[i].reshape(SSM_GROUPS, SSM_INNER // SSM_GROUPS))
        y_ssm = y_ssm.reshape(b, s, SSM_INNER).astype(h.dtype)
        qh = _rmsnorm(q.reshape(b, s, ATT_HEADS, ATT_HEAD_DIM), q_norm_g[i])
        kh = _rmsnorm(k.reshape(b, s, ATT_KV_HEADS, ATT_HEAD_DIM), k_norm_g[i])
        qh = _partial_rope(qh, cos, sin)
        kh = _partial_rope(kh, cos, sin)
        vh = v.reshape(b, s, ATT_KV_HEADS, ATT_HEAD_DIM)
        y_att = _swa_sinks(qh, kh, vh, sinks[i])
        h = h + jnp.concatenate([y_ssm, y_att], axis=-1) @ w_out[i]
        u = _rmsnorm(h, ffn_norm_g[i]) @ w_up[i]
        u = _causal_dwconv(u, ffn_conv_w[i], ffn_conv_b[i])
        g_ff, u_ff = jnp.split(u, [D_FF], axis=-1)
        h = h + (jax.nn.silu(g_ff) * u_ff) @ w_down[i]
        gate = jax.nn.sigmoid(_rmsnorm(h, ple_norm_g[i]) @ w_ple_gate[i])
        pe = _rmsnorm(p[i] @ w_ple_proj[i], ple_post_g[i])
        h = h + gate * pe
    return h
```

```python
import functools

import jax
import jax.numpy as jnp
import numpy as np
from jax import lax
from jax.experimental import pallas as pl
from jax.experimental.pallas import tpu as pltpu

F32 = jnp.float32
BF16 = jnp.bfloat16
EPS = 1e-6

D_MODEL = 2048
PLE_DIM = 256
HEAD_DIM = 64
SSM_HEADS = 32
SSM_GROUPS = 4
SSM_STATE = 128
SSM_CONV = 4
CHUNK = 128
SSM_INNER = SSM_HEADS * HEAD_DIM
BC_WIDTH = 2 * SSM_GROUPS * SSM_STATE
GROUP_WIDTH = SSM_INNER // SSM_GROUPS
ATT_HEADS = 32
ATT_KV_HEADS = 8
ATT_INNER = ATT_HEADS * HEAD_DIM
ATT_KV_DIM = ATT_KV_HEADS * HEAD_DIM
ROPE_THETA = 500000.0
ROPE_DIM = 16
D_FF = 5632
FFN_CONV = 3
PROJ_WIDTH = 2 * SSM_INNER + ATT_INNER + BC_WIDTH + 2 * ATT_KV_DIM

LANES = 128
SUBLANES = 8
VMEM_LIMIT = 56 * 1024 * 1024

NT_DIMS = (((1,), (1,)), ((), ()))


def _sigmoid(x):
    return 1.0 / (1.0 + jnp.exp(-x))


def _silu(x):
    return x * _sigmoid(x)


def _rmsnorm_rows(src_ref, g_ref, dst_ref, rows, step):
    def body(i, carry):
        r = pl.multiple_of(i * step, step)
        xf = src_ref[pl.ds(r, step), :].astype(F32)
        ms = jnp.mean(xf * xf, axis=-1, keepdims=True)
        dst_ref[pl.ds(r, step), :] = (xf * lax.rsqrt(ms + EPS) * g_ref[...]).astype(BF16)
        return carry
    lax.fori_loop(0, rows // step, body, 0)


def _inproj_kernel(x_ref, g_ref, w_ref, wdt_ref, proj_ref, dt_ref, a_scr, *, tm):
    @pl.when(pl.program_id(1) == 0)
    def _():
        _rmsnorm_rows(x_ref, g_ref, a_scr, tm, 256)
        dt_ref[...] = jnp.dot(a_scr[...], wdt_ref[...], preferred_element_type=F32)
    proj_ref[...] = jnp.dot(a_scr[...], w_ref[...], preferred_element_type=F32).astype(BF16)


def _in_proj(x, g, w, wdt, *, tm, tn):
    s = x.shape[0]
    return pl.pallas_call(
        functools.partial(_inproj_kernel, tm=tm),
        grid=(s // tm, PROJ_WIDTH // tn),
        in_specs=[
            pl.BlockSpec((tm, D_MODEL), lambda m, n: (m, 0)),
            pl.BlockSpec((1, D_MODEL), lambda m, n: (0, 0)),
            pl.BlockSpec((D_MODEL, tn), lambda m, n: (0, n)),
            pl.BlockSpec((D_MODEL, LANES), lambda m, n: (0, 0)),
        ],
        out_specs=[
            pl.BlockSpec((tm, tn), lambda m, n: (m, n)),
            pl.BlockSpec((tm, LANES), lambda m, n: (m, 0)),
        ],
        out_shape=[jax.ShapeDtypeStruct((s, PROJ_WIDTH), BF16),
                   jax.ShapeDtypeStruct((s, LANES), F32)],
        scratch_shapes=[pltpu.VMEM((tm, D_MODEL), BF16)],
        compiler_params=pltpu.CompilerParams(
            dimension_semantics=("arbitrary", "arbitrary"), vmem_limit_bytes=VMEM_LIMIT),
        name="in_proj",
    )(x, g, w, wdt)


def _ssd_kernel(z_ref, xs_ref, bc_ref, dt_ref, cwx_ref, cbx_ref, cwbc_ref, cbbc_ref,
                dtb_ref, a_ref, dsk_ref, ng_ref, e_ref, out_ref,
                xpad, bcpad, state, y_scr):
    L = CHUNK
    halo = SUBLANES

    @pl.when(pl.program_id(0) == 0)
    def _():
        xpad[0:halo, :] = jnp.zeros((halo, SSM_INNER), F32)
        bcpad[0:halo, :] = jnp.zeros((halo, BC_WIDTH), F32)
        state[...] = jnp.zeros_like(state)

    xpad[halo:halo + L, :] = xs_ref[...].astype(F32)
    bcpad[halo:halo + L, :] = bc_ref[...].astype(F32)

    def conv_silu(pad, w_ref, b_ref):
        first = halo - (SSM_CONV - 1)
        acc = b_ref[...] + w_ref[0:1, :] * pad[first:first + L, :]
        for k in range(1, SSM_CONV):
            acc = acc + w_ref[k:k + 1, :] * pad[first + k:first + k + L, :]
        return _silu(acc)

    xs = conv_silu(xpad, cwx_ref, cbx_ref)
    bc = conv_silu(bcpad, cwbc_ref, cbbc_ref)
    xpad[0:halo, :] = xpad[L:L + halo, :]
    bcpad[0:halo, :] = bcpad[L:L + halo, :]

    raw = dt_ref[...] + dtb_ref[...]
    dt = jnp.maximum(raw, 0.0) + jnp.log(1.0 + jnp.exp(-jnp.abs(raw)))
    adt = dt * a_ref[...]
    row = lax.broadcasted_iota(jnp.int32, (L, LANES), 0)
    lane = lax.broadcasted_iota(jnp.int32, (L, LANES), 1)
    acs = adt
    shift = 1
    while shift < L:
        acs = acs + jnp.where(row >= shift, pltpu.roll(acs, shift, axis=0), 0.0)
        shift *= 2
    acs_t = acs.T
    dt_t = dt.T
    ea = jnp.exp(acs)
    wdec = jnp.exp(acs[L - 1:L, :] - acs) * dt

    stack = jnp.concatenate([wdec, ea], axis=0)
    hi = stack.astype(BF16)
    lo = (stack - hi.astype(F32)).astype(BF16)
    ex = (jnp.dot(hi, e_ref[...], preferred_element_type=F32)
          + jnp.dot(lo, e_ref[...], preferred_element_type=F32))
    wexp = ex[0:L]
    eaexp = ex[L:2 * L]

    xw = (xs * wexp).astype(BF16)
    tril = row >= lane
    for g in range(SSM_GROUPS):
        bg = bc[:, g * SSM_STATE:(g + 1) * SSM_STATE]
        cg = bc[:, (SSM_GROUPS + g) * SSM_STATE:(SSM_GROUPS + g + 1) * SSM_STATE].astype(BF16)
        cb = lax.dot_general(cg, bg.astype(BF16), NT_DIMS, preferred_element_type=F32)
        for pr in range(GROUP_WIDTH // LANES):
            h0 = g * (SSM_HEADS // SSM_GROUPS) + 2 * pr
            ms = []
            for h in (h0, h0 + 1):
                seg = acs[:, h:h + 1] - acs_t[h:h + 1, :]
                dec = jnp.exp(jnp.where(tril, seg, -jnp.inf))
                ms.append((cb * dec * dt_t[h:h + 1, :]).astype(BF16))
            lhs = jnp.concatenate(ms, axis=1)
            xp = xs[:, h0 * HEAD_DIM:h0 * HEAD_DIM + LANES]
            rhs = jnp.concatenate([jnp.where(lane < HEAD_DIM, xp, 0.0),
                                   jnp.where(lane >= HEAD_DIM, xp, 0.0)], axis=0).astype(BF16)
            y_scr[:, h0 * HEAD_DIM:h0 * HEAD_DIM + LANES] = jnp.dot(
                lhs, rhs, preferred_element_type=F32)
        sl = slice(g * GROUP_WIDTH, (g + 1) * GROUP_WIDTH)
        prev = state[:, sl]
        yoff = jnp.dot(cg, prev.astype(BF16), preferred_element_type=F32) * eaexp[:, sl]
        y_scr[:, sl] = y_scr[:, sl] + yoff
        snew = jnp.dot(bg.T.astype(BF16), xw[:, sl], preferred_element_type=F32)
        state[:, sl] = prev * eaexp[L - 1:L, sl] + snew

    y = y_scr[...] + dsk_ref[...] * xs
    y = y * _silu(z_ref[...].astype(F32))
    for g in range(SSM_GROUPS):
        sl = slice(g * GROUP_WIDTH, (g + 1) * GROUP_WIDTH)
        yg = y[:, sl]
        msq = jnp.mean(yg * yg, axis=-1, keepdims=True)
        out_ref[:, sl] = (yg * lax.rsqrt(msq + EPS) * ng_ref[:, sl]).astype(BF16)


def _ssd(proj, dt_raw, cwx, cbx, cwbc, cbbc, dtb, a_neg, dsk, ng, expand):
    s = proj.shape[0]
    L = CHUNK
    const = lambda shape: pl.BlockSpec(shape, lambda c: (0, 0))
    return pl.pallas_call(
        _ssd_kernel,
        grid=(s // L,),
        in_specs=[
            pl.BlockSpec((L, SSM_INNER), lambda c: (c, 0)),
            pl.BlockSpec((L, SSM_INNER), lambda c: (c, 1)),
            pl.BlockSpec((L, BC_WIDTH), lambda c: (c, 3 * SSM_INNER // BC_WIDTH)),
            pl.BlockSpec((L, LANES), lambda c: (c, 0)),
            const((SSM_CONV, SSM_INNER)), const((1, SSM_INNER)),
            const((SSM_CONV, BC_WIDTH)), const((1, BC_WIDTH)),
            const((1, LANES)), const((1, LANES)),
            const((1, SSM_INNER)), const((1, SSM_INNER)),
            const((LANES, SSM_INNER)),
        ],
        out_specs=pl.BlockSpec((L, SSM_INNER), lambda c: (c, 0)),
        out_shape=jax.ShapeDtypeStruct((s, SSM_INNER), BF16),
        scratch_shapes=[
            pltpu.VMEM((L + 2 * SUBLANES, SSM_INNER), F32),
            pltpu.VMEM((L + 2 * SUBLANES, BC_WIDTH), F32),
            pltpu.VMEM((SSM_STATE, SSM_INNER), F32),
            pltpu.VMEM((L, SSM_INNER), F32),
        ],
        compiler_params=pltpu.CompilerParams(
            dimension_semantics=("arbitrary",), vmem_limit_bytes=VMEM_LIMIT),
        name="ssd",
    )(proj, proj, proj, dt_raw, cwx, cbx, cwbc, cbbc, dtb, a_neg, dsk, ng, expand)


def _swa_kernel(sinks_ref, q_ref, k_ref, v_ref, cos_ref, s1_ref, s2_ref, gq_ref, gk_ref, bd_ref,
                out_ref, kprev, vprev):
    L = CHUNK
    i = pl.program_id(0)

    @pl.when(i == 0)
    def _():
        kprev[...] = jnp.zeros_like(kprev)
        vprev[...] = jnp.zeros_like(vprev)

    cos = cos_ref[...]
    sin_lo = s1_ref[...]
    sin_hi = s2_ref[...]
    bd = bd_ref[...]
    row = lax.broadcasted_iota(jnp.int32, (L, LANES), 0)
    lane = lax.broadcasted_iota(jnp.int32, (L, LANES), 1)
    first_half = lane < HEAD_DIM
    tril = row >= lane
    tril2 = jnp.concatenate([tril, tril], axis=0)
    has_prev = i > 0
    half = ROPE_DIM // 2

    def norm_rope(t, g):
        ssq = jnp.dot((t * t).astype(BF16), bd, preferred_element_type=F32)
        tn = t * lax.rsqrt(ssq * (1.0 / HEAD_DIM) + EPS) * g
        return (tn * cos + pltpu.roll(tn, LANES - half, axis=1) * sin_lo
                + pltpu.roll(tn, half, axis=1) * sin_hi)

    for kb in range(ATT_KV_DIM // LANES):
        cs = slice(kb * LANES, (kb + 1) * LANES)
        kr = norm_rope(k_ref[:, cs].astype(F32), gk_ref[...])
        kr_sw = pltpu.roll(kr, HEAD_DIM, axis=1)
        vf = v_ref[:, cs].astype(F32)
        vf_sw = pltpu.roll(vf, HEAD_DIM, axis=1)
        for hf in range(2):
            j = 2 * kb + hf
            if hf == 0:
                kdup = jnp.where(first_half, kr, kr_sw).astype(BF16)
                vdup = jnp.where(first_half, vf, vf_sw).astype(BF16)
            else:
                kdup = jnp.where(first_half, kr_sw, kr).astype(BF16)
                vdup = jnp.where(first_half, vf_sw, vf).astype(BF16)
            kdup_prev = kprev[j]
            vcat = jnp.concatenate([vdup, vprev[j]], axis=0)
            for qb in (2 * j, 2 * j + 1):
                qs = slice(qb * LANES, (qb + 1) * LANES)
                qr = norm_rope(q_ref[:, qs].astype(F32), gq_ref[...]) * (HEAD_DIM ** -0.5)
                lhs = jnp.concatenate([jnp.where(first_half, qr, 0.0),
                                       jnp.where(first_half, 0.0, qr)], axis=0).astype(BF16)
                s_cur = lax.dot_general(lhs, kdup, NT_DIMS, preferred_element_type=F32)
                s_prev = lax.dot_general(lhs, kdup_prev, NT_DIMS, preferred_element_type=F32)
                s_prev = jnp.where(has_prev, s_prev, -jnp.inf)
                f = jnp.where(tril2, s_cur, s_prev)
                ps = []
                for hh in range(2):
                    sink = sinks_ref[2 * qb + hh]
                    fh = f[hh * L:(hh + 1) * L]
                    m = jnp.maximum(jnp.max(fh, axis=-1, keepdims=True), sink)
                    e = jnp.exp(fh - m)
                    den = jnp.sum(e, axis=-1, keepdims=True) + jnp.exp(sink - m)
                    p = e * (1.0 / den)
                    ps.append(jnp.concatenate([jnp.where(tril, p, 0.0),
                                               jnp.where(tril, 0.0, p)], axis=1))
                pcat = jnp.concatenate(ps, axis=0).astype(BF16)
                o = jnp.dot(pcat, vcat, preferred_element_type=F32)
                out_ref[:, qs] = jnp.where(first_half, o[0:L], o[L:2 * L]).astype(BF16)
            kprev[j] = kdup
            vprev[j] = vdup


def _swa(proj, sinks, cos_t, sin_lo_t, sin_hi_t, gq, gk, bd):
    s = proj.shape[0]
    L = CHUNK
    const = lambda shape: pl.BlockSpec(shape, lambda i: (0, 0))
    return pl.pallas_call(
        _swa_kernel,
        grid=(s // L,),
        in_specs=[
            pl.BlockSpec(memory_space=pltpu.SMEM),
            pl.BlockSpec((L, ATT_INNER), lambda i: (i, 2)),
            pl.BlockSpec((L, ATT_KV_DIM), lambda i: (i, (PROJ_WIDTH - 2 * ATT_KV_DIM) // ATT_KV_DIM)),
            pl.BlockSpec((L, ATT_KV_DIM), lambda i: (i, (PROJ_WIDTH - ATT_KV_DIM) // ATT_KV_DIM)),
            pl.BlockSpec((L, LANES), lambda i: (i, 0)),
            pl.BlockSpec((L, LANES), lambda i: (i, 0)),
            pl.BlockSpec((L, LANES), lambda i: (i, 0)),
            const((1, LANES)), const((1, LANES)), const((LANES, LANES)),
        ],
        out_specs=pl.BlockSpec((L, ATT_INNER), lambda i: (i, 0)),
        out_shape=jax.ShapeDtypeStruct((s, ATT_INNER), BF16),
        scratch_shapes=[
            pltpu.VMEM((ATT_KV_HEADS, L, LANES), BF16),
            pltpu.VMEM((ATT_KV_HEADS, L, LANES), BF16),
        ],
        compiler_params=pltpu.CompilerParams(
            dimension_semantics=("arbitrary",), vmem_limit_bytes=VMEM_LIMIT),
        name="swa",
    )(sinks, proj, proj, proj, cos_t, sin_lo_t, sin_hi_t, gq, gk, bd)


def _outproj_kernel(x_ref, ys_ref, ya_ref, ws_ref, wa_ref, out_ref):
    out_ref[...] = (x_ref[...]
                    + jnp.dot(ys_ref[...], ws_ref[...], preferred_element_type=F32)
                    + jnp.dot(ya_ref[...], wa_ref[...], preferred_element_type=F32))


def _out_proj(x, y_ssm, y_att, w_out, *, tm, tn):
    s = x.shape[0]
    return pl.pallas_call(
        _outproj_kernel,
        grid=(s // tm, D_MODEL // tn),
        in_specs=[
            pl.BlockSpec((tm, tn), lambda m, n: (m, n)),
            pl.BlockSpec((tm, SSM_INNER), lambda m, n: (m, 0)),
            pl.BlockSpec((tm, ATT_INNER), lambda m, n: (m, 0)),
            pl.BlockSpec((SSM_INNER, tn), lambda m, n: (0, n)),
            pl.BlockSpec((ATT_INNER, tn), lambda m, n: (SSM_INNER // ATT_INNER, n)),
        ],
        out_specs=pl.BlockSpec((tm, tn), lambda m, n: (m, n)),
        out_shape=jax.ShapeDtypeStruct((s, D_MODEL), F32),
        compiler_params=pltpu.CompilerParams(
            dimension_semantics=("arbitrary", "arbitrary"), vmem_limit_bytes=VMEM_LIMIT),
        name="out_proj",
    )(x, y_ssm, y_att, w_out, w_out)


def _ffn_kernel(h_ref, g_ref, wg_ref, wu_ref, cwg_ref, cwu_ref, cbg_ref, cbu_ref, wd_ref,
                out_ref, a_scr, gpad, upad, gcarry, ucarry, *, tm):
    m = pl.program_id(0)
    j = pl.program_id(1)
    halo = SUBLANES

    @pl.when(j == 0)
    def _():
        _rmsnorm_rows(h_ref, g_ref, a_scr, tm, 256)
        out_ref[...] = h_ref[...]

    def conv(pad, carry, w_ref, b_ref, w_up_ref):
        @pl.when(m == 0)
        def _():
            pad[0:halo, :] = jnp.zeros((halo, pad.shape[1]), F32)

        @pl.when(m > 0)
        def _():
            pad[0:halo, :] = carry[j]

        pad[halo:halo + tm, :] = jnp.dot(a_scr[...], w_up_ref[...], preferred_element_type=F32)
        carry[j] = pad[tm:tm + halo, :]
        first = halo - (FFN_CONV - 1)
        acc = b_ref[...] + w_ref[0:1, :] * pad[first:first + tm, :]
        for k in range(1, FFN_CONV):
            acc = acc + w_ref[k:k + 1, :] * pad[first + k:first + k + tm, :]
        return acc

    gate = conv(gpad, gcarry, cwg_ref, cbg_ref, wg_ref)
    up = conv(upad, ucarry, cwu_ref, cbu_ref, wu_ref)
    act = (_silu(gate) * up).astype(BF16)
    out_ref[...] += jnp.dot(act, wd_ref[...], preferred_element_type=F32)


def _ffn(h1, g, w_up, conv_w, conv_b, w_down, *, tm, tf):
    s = h1.shape[0]
    nj = D_FF // tf
    return pl.pallas_call(
        functools.partial(_ffn_kernel, tm=tm),
        grid=(s // tm, nj),
        in_specs=[
            pl.BlockSpec((tm, D_MODEL), lambda m, j: (m, 0)),
            pl.BlockSpec((1, D_MODEL), lambda m, j: (0, 0)),
            pl.BlockSpec((D_MODEL, tf), lambda m, j: (0, j)),
            pl.BlockSpec((D_MODEL, tf), lambda m, j: (0, nj + j)),
            pl.BlockSpec((FFN_CONV, tf), lambda m, j: (0, j)),
            pl.BlockSpec((FFN_CONV, tf), lambda m, j: (0, nj + j)),
            pl.BlockSpec((1, tf), lambda m, j: (0, j)),
            pl.BlockSpec((1, tf), lambda m, j: (0, nj + j)),
            pl.BlockSpec((tf, D_MODEL), lambda m, j: (j, 0)),
        ],
        out_specs=pl.BlockSpec((tm, D_MODEL), lambda m, j: (m, 0)),
        out_shape=jax.ShapeDtypeStruct((s, D_MODEL), F32),
        scratch_shapes=[
            pltpu.VMEM((tm, D_MODEL), BF16),
            pltpu.VMEM((tm + SUBLANES, tf), F32),
            pltpu.VMEM((tm + SUBLANES, tf), F32),
            pltpu.VMEM((nj, SUBLANES, tf), F32),
            pltpu.VMEM((nj, SUBLANES, tf), F32),
        ],
        compiler_params=pltpu.CompilerParams(
            dimension_semantics=("arbitrary", "arbitrary"), vmem_limit_bytes=VMEM_LIMIT),
        name="ffn",
    )(h1, g, w_up, w_up, conv_w, conv_w, conv_b, conv_b, w_down)


def _ple_kernel(h_ref, p_ref, g_ref, wpg_ref, wpe_ref, pg_ref, out_ref, a_scr, *, tm):
    _rmsnorm_rows(h_ref, g_ref, a_scr, tm, 256)
    gate = _sigmoid(jnp.dot(a_scr[...], wpg_ref[...], preferred_element_type=F32))
    pe = jnp.dot(p_ref[...].astype(BF16), wpe_ref[...], preferred_element_type=F32)
    ms = jnp.mean(pe * pe, axis=-1, keepdims=True)
    pe = pe * lax.rsqrt(ms + EPS) * pg_ref[...]
    out_ref[...] = h_ref[...] + gate * pe


def _ple(h2, p, g, wpg, wpe, pg, *, tm):
    s = h2.shape[0]
    return pl.pallas_call(
        functools.partial(_ple_kernel, tm=tm),
        grid=(s // tm,),
        in_specs=[
            pl.BlockSpec((tm, D_MODEL), lambda m: (m, 0)),
            pl.BlockSpec((tm, PLE_DIM), lambda m: (m, 0)),
            pl.BlockSpec((1, D_MODEL), lambda m: (0, 0)),
            pl.BlockSpec((D_MODEL, D_MODEL), lambda m: (0, 0)),
            pl.BlockSpec((PLE_DIM, D_MODEL), lambda m: (0, 0)),
            pl.BlockSpec((1, D_MODEL), lambda m: (0, 0)),
        ],
        out_specs=pl.BlockSpec((tm, D_MODEL), lambda m: (m, 0)),
        out_shape=jax.ShapeDtypeStruct((s, D_MODEL), F32),
        scratch_shapes=[pltpu.VMEM((tm, D_MODEL), BF16)],
        compiler_params=pltpu.CompilerParams(
            dimension_semantics=("arbitrary",), vmem_limit_bytes=VMEM_LIMIT),
        name="ple",
    )(h2, p, g, wpg, wpe, pg)


def _rope_tables(pos):
    half = ROPE_DIM // 2
    inv_freq = ROPE_THETA ** (-jnp.arange(0, ROPE_DIM, 2, dtype=F32) / ROPE_DIM)
    ang = pos.astype(F32)[:, None] * inv_freq
    cos, sin = jnp.cos(ang), jnp.sin(ang)
    s = pos.shape[0]
    ones = jnp.ones((s, HEAD_DIM - ROPE_DIM), F32)
    zeros_tail = jnp.zeros((s, HEAD_DIM - ROPE_DIM), F32)
    zeros_half = jnp.zeros((s, half), F32)
    cos_h = jnp.concatenate([cos, cos, ones], axis=1)
    lo_h = jnp.concatenate([-sin, zeros_half, zeros_tail], axis=1)
    hi_h = jnp.concatenate([zeros_half, sin, zeros_tail], axis=1)
    tile2 = lambda t: jnp.concatenate([t, t], axis=1)
    return tile2(cos_h), tile2(lo_h), tile2(hi_h)


def _layer(h, p, pos, attn_norm_g, w_in, conv_w, conv_b, dt_bias, a_log, d_skip, ssm_norm_g,
           q_norm_g, k_norm_g, sinks, w_out, ffn_norm_g, w_up, ffn_conv_w, ffn_conv_b, w_down,
           ple_norm_g, w_ple_gate, w_ple_proj, ple_post_g, *, tm, tn, tf):
    row = lambda v: v.astype(F32).reshape(1, -1)
    o1 = SSM_INNER
    o2 = o1 + SSM_INNER
    o3 = o2 + BC_WIDTH
    o4 = o3 + SSM_HEADS
    o5 = o4 + ATT_INNER
    o6 = o5 + ATT_KV_DIM
    w_main = jnp.concatenate([w_in[:, :o2], w_in[:, o4:o5], w_in[:, o2:o3], w_in[:, o5:]],
                             axis=1).astype(BF16)
    w_dt = jnp.pad(w_in[:, o3:o4], ((0, 0), (0, LANES - SSM_HEADS))).astype(BF16)
    pad_heads = lambda v: jnp.pad(v.astype(F32), (0, LANES - SSM_HEADS)).reshape(1, LANES)
    expand = (jnp.arange(LANES)[:, None] == (jnp.arange(SSM_INNER) // HEAD_DIM)[None, :]).astype(BF16)
    lane = jnp.arange(LANES)
    bd = ((lane[:, None] // HEAD_DIM) == (lane[None, :] // HEAD_DIM)).astype(BF16)
    tile2 = lambda v: jnp.concatenate([v, v]).astype(F32).reshape(1, LANES)
    cos_t, sin_lo_t, sin_hi_t = _rope_tables(pos)

    proj, dt_raw = _in_proj(h, row(attn_norm_g), w_main, w_dt, tm=tm, tn=tn)
    y_ssm = _ssd(proj, dt_raw,
                 conv_w[:, :SSM_INNER], row(conv_b[:SSM_INNER]),
                 conv_w[:, SSM_INNER:], row(conv_b[SSM_INNER:]),
                 pad_heads(dt_bias), pad_heads(-jnp.exp(a_log.astype(F32))),
                 row(jnp.repeat(d_skip, HEAD_DIM)), row(ssm_norm_g), expand)
    y_att = _swa(proj, sinks.astype(F32), cos_t, sin_lo_t, sin_hi_t,
                 tile2(q_norm_g), tile2(k_norm_g), bd)
    h1 = _out_proj(h, y_ssm, y_att, w_out.astype(BF16), tm=tm, tn=tn)
    h2 = _ffn(h1, row(ffn_norm_g), w_up.astype(BF16), ffn_conv_w, row(ffn_conv_b),
              w_down.astype(BF16), tm=min(tm, 512), tf=tf)
    return _ple(h2, p, row(ple_norm_g), w_ple_gate.astype(BF16), w_ple_proj.astype(BF16),
                row(ple_post_g), tm=min(tm, 512))


def kernel(x, p, positions, attn_norm_g, w_in, conv_w, conv_b, dt_bias, a_log, d_skip, ssm_norm_g,
           q_norm_g, k_norm_g, sinks, w_out, ffn_norm_g, w_up, ffn_conv_w, ffn_conv_b, w_down,
           ple_norm_g, w_ple_gate, w_ple_proj, ple_post_g):
    batch, depth = x.shape[0], p.shape[0]
    tm, tn, tf = min(1024, x.shape[1]), 1024, 512
    outs = []
    for b in range(batch):
        h = x[b]
        for i in range(depth):
            h = _layer(h, p[i, b], positions[b], attn_norm_g[i], w_in[i], conv_w[i], conv_b[i],
                       dt_bias[i], a_log[i], d_skip[i], ssm_norm_g[i], q_norm_g[i], k_norm_g[i],
                       sinks[i], w_out[i], ffn_norm_g[i], w_up[i], ffn_conv_w[i], ffn_conv_b[i],
                       w_down[i], ple_norm_g[i], w_ple_gate[i], w_ple_proj[i], ple_post_g[i],
                       tm=tm, tn=tn, tf=tf)
        outs.append(h)
    return jnp.stack(outs, axis=0)
```
